```python
import math
import jax, jax.numpy as jnp
from jax import lax
import numpy as np

D_MODEL = 4096
BATCH = 4
SEQ = 4096
DEPTH = 4

N_A_LAYERS = DEPTH // 2
N_B_LAYERS = DEPTH - N_A_LAYERS
PLE_DIM = 256
SGU_EXPAND = 2
SGU_WIDTH = SGU_EXPAND * D_MODEL
SGU_GROUPS = 16
SGU_GROUP_DIM = SGU_WIDTH // SGU_GROUPS
CHUNK = 128
DIFF_HEADS = 16
DIFF_HEAD_DIM = D_MODEL // DIFF_HEADS // 2
DIFF_V_DIM = 2 * DIFF_HEAD_DIM
DIFF_QK_WIDTH = DIFF_HEADS * 2 * DIFF_HEAD_DIM
DIFF_WIDTH = DIFF_HEADS * DIFF_V_DIM
Q_BLOCK = 128
RMS_EPS = 1e-6
LN_EPS = 1e-5
NEG_INF = -1e30

kernel_name = "yoco_sgu_diffattn_hybrid"


def rms_norm(x, g):
    xf = x.astype(jnp.float32)
    y = xf * lax.rsqrt(jnp.mean(xf * xf, axis=-1, keepdims=True) + RMS_EPS)
    return (y * g.astype(jnp.float32)).astype(x.dtype)


def layer_norm(x, g, b):
    xf = x.astype(jnp.float32)
    mu = jnp.mean(xf, axis=-1, keepdims=True)
    var = jnp.mean(jnp.square(xf - mu), axis=-1, keepdims=True)
    y = (xf - mu) * lax.rsqrt(var + LN_EPS)
    return (y * g.astype(jnp.float32) + b.astype(jnp.float32)).astype(x.dtype)


def lambda_init_fn(layer_idx):
    return 0.8 - 0.6 * math.exp(-0.3 * layer_idx)


def sgu_mixer(h, w_in, ln_g, ln_b, w_s, b_s, w_out):
    bsz, seq, _ = h.shape
    z = h @ w_in
    u, v, gate = jnp.split(z, 3, axis=-1)
    u = jax.nn.gelu(u)
    v = layer_norm(jax.nn.gelu(v), ln_g, ln_b)
    v = v.reshape(bsz, seq // CHUNK, CHUNK, SGU_GROUPS, SGU_GROUP_DIM)
    w_causal = w_s * jnp.tril(jnp.ones((CHUNK, CHUNK), w_s.dtype))
    mixed = jnp.einsum('gij,bnjgc->bnigc', w_causal, v) + b_s.T[:, :, None]
    mixed = mixed.reshape(bsz, seq, SGU_WIDTH)
    y = u * mixed * jax.nn.silu(gate)
    return y @ w_out


def shared_kv(h, kv_norm_g, w_kv, k_norm_g):
    bsz, seq, _ = h.shape
    kv = rms_norm(h, kv_norm_g) @ w_kv
    k, v = jnp.split(kv, [DIFF_QK_WIDTH], axis=-1)
    k = rms_norm(k.reshape(bsz, seq, DIFF_HEADS, 2, DIFF_HEAD_DIM), k_norm_g)
    v = v.reshape(bsz, seq, DIFF_HEADS, DIFF_V_DIM)
    return k, v


def diff_attn_mixer(h, k, v, w_in, q_norm_g, lam_q1, lam_k1, lam_q2, lam_k2, sub_g, w_out, lambda_init):
    bsz, seq, _ = h.shape
    z = h @ w_in
    q, gate = jnp.split(z, [DIFF_QK_WIDTH], axis=-1)
    q = rms_norm(q.reshape(bsz, seq, DIFF_HEADS, 2, DIFF_HEAD_DIM), q_norm_g)
    f32 = jnp.float32
    lam = (jnp.exp(jnp.sum(lam_q1.astype(f32) * lam_k1.astype(f32)))
           - jnp.exp(jnp.sum(lam_q2.astype(f32) * lam_k2.astype(f32))) + lambda_init)
    n_blk = seq // Q_BLOCK
    q_blocks = q.reshape(bsz, n_blk, Q_BLOCK, DIFF_HEADS, 2, DIFF_HEAD_DIM).transpose(1, 0, 2, 3, 4, 5)
    scale = DIFF_HEAD_DIM ** -0.5
    k_pos = jnp.arange(seq)

    def one_block(args):
        qb, blk = args
        s = jnp.einsum('bqhcd,bkhcd->bhcqk', qb, k).astype(f32) * scale
        q_pos = blk * Q_BLOCK + jnp.arange(Q_BLOCK)
        mask = k_pos[None, :] <= q_pos[:, None]
        s = jnp.where(mask, s, NEG_INF)
        a = jax.nn.softmax(s, axis=-1)
        w = a[:, :, 0] - lam * a[:, :, 1]
        return jnp.einsum('bhqk,bkhd->bqhd', w.astype(v.dtype), v)

    o = lax.map(one_block, (q_blocks, jnp.arange(n_blk)))
    o = o.transpose(1, 0, 2, 3, 4).reshape(bsz, seq, DIFF_HEADS, DIFF_V_DIM)
    o = rms_norm(o, sub_g) * (1.0 - lambda_init)
    o = o.reshape(bsz, seq, DIFF_WIDTH) * jax.nn.silu(gate)
    return o @ w_out


def per_layer_embed(h, p_i, w_ple, gate_g, gate_w):
    gate = jax.nn.sigmoid(rms_norm(h, gate_g) @ gate_w)
    return h + gate * (p_i @ w_ple)


def setup_inputs(seed: int = 0) -> dict:
    key = jax.random.key(seed)
    ks = jax.random.split(key, 24)
    f32 = jnp.float32
    D, E = D_MODEL, SGU_WIDTH
    nrm = lambda k, shape, s: jax.random.normal(k, shape, f32) * s
    gain = lambda k, shape: 1.0 + 0.02 * jax.random.normal(k, shape, f32)
    return {
        "x": nrm(ks[0], (BATCH, SEQ, D), 1.0),
        "p": nrm(ks[1], (DEPTH, BATCH, SEQ, PLE_DIM), 1.0),
        "a_norm_g": gain(ks[2], (N_A_LAYERS, D)),
        "a_w_in": nrm(ks[3], (N_A_LAYERS, D, 3 * E), D ** -0.5),
        "a_ln_g": gain(ks[4], (N_A_LAYERS, E)),
        "a_ln_b": nrm(ks[5], (N_A_LAYERS, E), 0.02),
        "a_w_s": nrm(ks[6], (N_A_LAYERS, SGU_GROUPS, CHUNK, CHUNK), CHUNK ** -0.5),
        "a_b_s": 1.0 + nrm(ks[7], (N_A_LAYERS, SGU_GROUPS, CHUNK), 0.1),
        "a_w_out": nrm(ks[8], (N_A_LAYERS, E, D), E ** -0.5),
        "kv_norm_g": gain(ks[9], (D,)),
        "w_kv": nrm(ks[10], (D, DIFF_QK_WIDTH + DIFF_WIDTH), D ** -0.5),
        "k_norm_g": gain(ks[11], (DIFF_HEAD_DIM,)),
        "b_norm_g": gain(ks[12], (N_B_LAYERS, D)),
        "b_w_in": nrm(ks[13], (N_B_LAYERS, D, DIFF_QK_WIDTH + DIFF_WIDTH), D ** -0.5),
        "b_q_norm_g": gain(ks[14], (N_B_LAYERS, DIFF_HEAD_DIM)),
        "b_lam_q1": nrm(ks[15], (N_B_LAYERS, DIFF_HEAD_DIM), 0.1),
        "b_lam_k1": nrm(ks[16], (N_B_LAYERS, DIFF_HEAD_DIM), 0.1),
        "b_lam_q2": nrm(ks[17], (N_B_LAYERS, DIFF_HEAD_DIM), 0.1),
        "b_lam_k2": nrm(ks[18], (N_B_LAYERS, DIFF_HEAD_DIM), 0.1),
        "b_sub_norm_g": gain(ks[19], (N_B_LAYERS, DIFF_V_DIM)),
        "b_w_out": nrm(ks[20], (N_B_LAYERS, DIFF_WIDTH, D), DIFF_WIDTH ** -0.5),
        "ple_w": nrm(ks[21], (DEPTH, PLE_DIM, D), PLE_DIM ** -0.5),
        "ple_gate_norm_g": gain(ks[22], (DEPTH, D)),
        "ple_gate_w": nrm(ks[23], (DEPTH, D, D), D ** -0.5),
    }


def reference(x, p, a_norm_g, a_w_in, a_ln_g, a_ln_b, a_w_s, a_b_s, a_w_out,
              kv_norm_g, w_kv, k_norm_g, b_norm_g, b_w_in, b_q_norm_g,
              b_lam_q1, b_lam_k1, b_lam_q2, b_lam_k2, b_sub_norm_g, b_w_out,
              ple_w, ple_gate_norm_g, ple_gate_w):
    h = x
    k_sh = v_sh = None
    for i in range(DEPTH):
        if i < N_A_LAYERS:
            h = h + sgu_mixer(rms_norm(h, a_norm_g[i]), a_w_in[i], a_ln_g[i], a_ln_b[i],
                              a_w_s[i], a_b_s[i], a_w_out[i])
        else:
            if i == N_A_LAYERS:
                k_sh, v_sh = shared_kv(h, kv_norm_g, w_kv, k_norm_g)
            j = i - N_A_LAYERS
            h = h + diff_attn_mixer(rms_norm(h, b_norm_g[j]), k_sh, v_sh, b_w_in[j], b_q_norm_g[j],
                                    b_lam_q1[j], b_lam_k1[j], b_lam_q2[j], b_lam_k2[j],
                                    b_sub_norm_g[j], b_w_out[j], lambda_init_fn(i))
        h = per_layer_embed(h, p[i], ple_w[i], ple_gate_norm_g[i], ple_gate_w[i])
    return h
```

```python
import functools
import math

import jax
import jax.numpy as jnp
from jax import lax
from jax.experimental import pallas as pl
from jax.experimental.pallas import tpu as pltpu

F32 = jnp.float32
BF16 = jnp.bfloat16

LANES = 128
V7X_VMEM_BYTES = 64 * 1024 * 1024
VMEM_LIMIT_CAP = 56 * 1024 * 1024

CHUNK = 128
HEAD_DIM = 128
V_DIM = 2 * HEAD_DIM
RMS_EPS = 1e-6
LN_EPS = 1e-5
NEG_INF = -1e30


def _lambda_init(layer_idx):
    return 0.8 - 0.6 * math.exp(-0.3 * layer_idx)


def _tile(dim, pref):
    t = min(dim, pref)
    assert dim % t == 0, (dim, pref)
    return t


def _params(block_bytes, temp_bytes, semantics):
    need = 2 * block_bytes + temp_bytes + (4 << 20)
    return pltpu.CompilerParams(
        dimension_semantics=semantics,
        vmem_limit_bytes=int(min(max(need, 16 << 20), VMEM_LIMIT_CAP)))


def _row_rstd(ssq_ref, width):
    return lax.rsqrt(ssq_ref[:, :1] * (1.0 / width) + RMS_EPS)


def _prep_kernel(x_ref, g_ref, hb_ref, ssq_ref):
    x = x_ref[...]
    hb_ref[...] = (x * g_ref[...]).astype(BF16)
    ssq_ref[...] = jnp.broadcast_to(jnp.sum(x * x, axis=-1, keepdims=True), ssq_ref.shape)


def _prep_call(x, g):
    m, d = x.shape
    tm = _tile(m, 256)
    blk = tm * d * 6 + tm * LANES * 4
    return pl.pallas_call(
        _prep_kernel,
        grid=(m // tm,),
        in_specs=[pl.BlockSpec((tm, d), lambda i: (i, 0)),
                  pl.BlockSpec((1, d), lambda i: (0, 0))],
        out_specs=[pl.BlockSpec((tm, d), lambda i: (i, 0)),
                   pl.BlockSpec((tm, LANES), lambda i: (i, 0))],
        out_shape=[jax.ShapeDtypeStruct((m, d), BF16),
                   jax.ShapeDtypeStruct((m, LANES), F32)],
        compiler_params=_params(blk, tm * d * 8, ("arbitrary",)),
        name="prep_norm",
    )(x, g)


def _inproj_kernel(kind, d_in, scale, hb_ref, ssq_ref, w_ref, *rest):
    z = jnp.dot(hb_ref[...], w_ref[...], preferred_element_type=F32)
    z = z * _row_rstd(ssq_ref, d_in)
    if kind == "gelu":
        (o_ref,) = rest
        o_ref[...] = jax.nn.gelu(z).astype(BF16)
    elif kind == "gelu_stats":
        o_ref, s1_ref, s2_ref = rest
        g = jax.nn.gelu(z)
        o_ref[...] = g.astype(BF16)

        @pl.when(pl.program_id(1) == 0)
        def _():
            s1_ref[...] = jnp.zeros_like(s1_ref)
            s2_ref[...] = jnp.zeros_like(s2_ref)

        s1_ref[...] += jnp.sum(g, axis=-1, keepdims=True)
        s2_ref[...] += jnp.sum(g * g, axis=-1, keepdims=True)
    elif kind == "silu":
        (o_ref,) = rest
        o_ref[...] = (z * jax.nn.sigmoid(z)).astype(BF16)
    elif kind == "headnorm":
        g_ref, o_ref = rest
        gain = g_ref[...] * scale
        for c in range(z.shape[1] // HEAD_DIM):
            zc = z[:, c * HEAD_DIM:(c + 1) * HEAD_DIM]
            ms = jnp.mean(zc * zc, axis=-1, keepdims=True)
            o_ref[:, c * HEAD_DIM:(c + 1) * HEAD_DIM] = (
                zc * lax.rsqrt(ms + RMS_EPS) * gain).astype(BF16)
    else:
        assert kind == "cast"
        (o_ref,) = rest
        o_ref[...] = z.astype(BF16)


def _inproj_call(hb, ssq, w, layer, col0, ncols, kind, head_gain=None, scale=1.0, name=""):
    m, d = hb.shape
    tm = _tile(m, 1024)
    tn = _tile(ncols, 512)
    assert col0 % tn == 0
    off = col0 // tn
    if w.ndim == 3:
        w_spec = pl.BlockSpec((None, d, tn), lambda i, j: (layer, 0, j + off))
    else:
        w_spec = pl.BlockSpec((d, tn), lambda i, j: (0, j + off))
    in_specs = [pl.BlockSpec((tm, d), lambda i, j: (i, 0)),
                pl.BlockSpec((tm, LANES), lambda i, j: (i, 0)),
                w_spec]
    args = [hb, ssq, w]
    out_specs = [pl.BlockSpec((tm, tn), lambda i, j: (i, j))]
    out_shape = [jax.ShapeDtypeStruct((m, ncols), BF16)]
    if kind == "headnorm":
        in_specs.append(pl.BlockSpec((1, HEAD_DIM), lambda i, j: (0, 0)))
        args.append(head_gain)
    if kind == "gelu_stats":
        out_specs += [pl.BlockSpec((tm, LANES), lambda i, j: (i, 0))] * 2
        out_shape += [jax.ShapeDtypeStruct((m, LANES), F32)] * 2
    blk = tm * d * 2 + d * tn * 2 + tm * tn * 2 + 3 * tm * LANES * 4
    res = pl.pallas_call(
        functools.partial(_inproj_kernel, kind, d, scale),
        grid=(m // tm, ncols // tn),
        in_specs=in_specs,
        out_specs=out_specs,
        out_shape=out_shape,
        compiler_params=_params(blk, 4 * tm * tn * 4, ("parallel", "arbitrary")),
        name=name or ("inproj_" + kind),
    )(*args)
    return res if kind == "gelu_stats" else res[0]


def _sgu_mix_kernel(width, gu_ref, gv_ref, sg_ref, s1_ref, s2_ref, lng_ref, lnb_ref,
                    ws_ref, bs_ref, y_ref):
    tr, tc = gv_ref.shape
    gdim = tc // ws_ref.shape[0]
    mu = s1_ref[:, :1] * (1.0 / width)
    var = jnp.maximum(s2_ref[:, :1] * (1.0 / width) - mu * mu, 0.0)
    rstd = lax.rsqrt(var + LN_EPS)
    row = lax.broadcasted_iota(jnp.int32, (CHUNK, CHUNK), 0)
    col = lax.broadcasted_iota(jnp.int32, (CHUNK, CHUNK), 1)
    for g in range(ws_ref.shape[0]):
        cs = slice(g * gdim, (g + 1) * gdim)
        vn = ((gv_ref[:, cs].astype(F32) - mu) * rstd * lng_ref[:, cs]
              + lnb_ref[:, cs]).astype(BF16)
        wc = jnp.where(col <= row, ws_ref[g], 0.0).astype(BF16)
        for c in range(tr // CHUNK):
            rs = slice(c * CHUNK, (c + 1) * CHUNK)
            mixed = jnp.dot(wc, vn[rs], preferred_element_type=F32) + bs_ref[g]
            y_ref[rs, cs] = (gu_ref[rs, cs].astype(F32) * mixed
                             * sg_ref[rs, cs].astype(F32)).astype(BF16)


def _sgu_mix_call(gu, gv, sg, s1, s2, ln_g, ln_b, w_s, b_s, name):
    m, e = gv.shape
    groups = w_s.shape[0]
    gdim = e // groups
    gc = 2 if groups % 2 == 0 else 1
    tr = _tile(m, 512)
    tc = gc * gdim
    act = pl.BlockSpec((tr, tc), lambda i, j: (i, j))
    stat = pl.BlockSpec((tr, LANES), lambda i, j: (i, 0))
    vec = pl.BlockSpec((1, tc), lambda i, j: (0, j))
    blk = 4 * tr * tc * 2 + 2 * tr * LANES * 4 + 2 * tc * 4 + 2 * gc * CHUNK * LANES * 4
    return pl.pallas_call(
        functools.partial(_sgu_mix_kernel, e),
        grid=(m // tr, groups // gc),
        in_specs=[act, act, act, stat, stat, vec, vec,
                  pl.BlockSpec((gc, CHUNK, CHUNK), lambda i, j: (j, 0, 0)),
                  pl.BlockSpec((gc, CHUNK, 1), lambda i, j: (j, 0, 0))],
        out_specs=act,
        out_shape=jax.ShapeDtypeStruct((m, e), BF16),
        compiler_params=_params(blk, 6 * tr * tc * 4, ("parallel", "arbitrary")),
        name=name,
    )(gu, gv, sg, s1, s2, ln_g, ln_b, w_s, b_s)


def _emit_residual(hn, gain_refs, hnew_ref, hb_refs, ssq_ref):
    hnew_ref[...] = hn
    for g_ref, hb_ref in zip(gain_refs, hb_refs):
        hb_ref[...] = (hn * g_ref[...]).astype(BF16)
    if ssq_ref is not None:
        @pl.when(pl.program_id(1) == 0)
        def _():
            ssq_ref[...] = jnp.zeros_like(ssq_ref)

        ssq_ref[...] += jnp.sum(hn * hn, axis=-1, keepdims=True)


def _residual_specs(m, d, tm, tn, n_gain):
    gain_specs = [pl.BlockSpec((1, tn), lambda i, j: (0, j))] * n_gain
    out_specs = [pl.BlockSpec((tm, tn), lambda i, j: (i, j))] * (1 + n_gain)
    out_shape = [jax.ShapeDtypeStruct((m, d), F32)]
    out_shape += [jax.ShapeDtypeStruct((m, d), BF16)] * n_gain
    if n_gain:
        out_specs.append(pl.BlockSpec((tm, LANES), lambda i, j: (i, 0)))
        out_shape.append(jax.ShapeDtypeStruct((m, LANES), F32))
    return gain_specs, out_specs, out_shape


def _split_residual_refs(rest, n_gain):
    gain_refs = rest[:n_gain]
    hnew_ref = rest[n_gain]
    hb_refs = rest[n_gain + 1:2 * n_gain + 1]
    ssq_ref = rest[2 * n_gain + 1] if n_gain else None
    return gain_refs, hnew_ref, hb_refs, ssq_ref


def _outproj_kernel(n_gain, y_ref, w_ref, h_ref, *rest):
    hn = h_ref[...] + jnp.dot(y_ref[...], w_ref[...], preferred_element_type=F32)
    _emit_residual(hn, *_split_residual_refs(rest, n_gain))


def _outproj_call(y, w, layer, h, gains, name):
    m, k = y.shape
    d = h.shape[1]
    tm = _tile(m, 1024 if k <= 4096 else 512)
    tn = _tile(d, 512)
    n_gain = len(gains)
    gain_specs, out_specs, out_shape = _residual_specs(m, d, tm, tn, n_gain)
    blk = tm * k * 2 + k * tn * 2 + tm * tn * (8 + 2 * n_gain) + tm * LANES * 4
    return pl.pallas_call(
        functools.partial(_outproj_kernel, n_gain),
        grid=(m // tm, d // tn),
        in_specs=[pl.BlockSpec((tm, k), lambda i, j: (i, 0)),
                  pl.BlockSpec((None, k, tn), lambda i, j: (layer, 0, j)),
                  pl.BlockSpec((tm, tn), lambda i, j: (i, j))] + gain_specs,
        out_specs=out_specs,
        out_shape=out_shape,
        compiler_params=_params(blk, 3 * tm * tn * 4, ("parallel", "arbitrary")),
        name=name,
    )(y, w, h, *gains)


def _ple_kernel(n_gain, d_in, hb_ref, ssq_ref, gw_ref, p_ref, pw_ref, h_ref, *rest):
    logits = jnp.dot(hb_ref[...], gw_ref[...], preferred_element_type=F32)
    gate = jax.nn.sigmoid(logits * _row_rstd(ssq_ref, d_in))
    emb = jnp.dot(p_ref[...].astype(BF16), pw_ref[...], preferred_element_type=F32)
    hn = h_ref[...] + gate * emb
    _emit_residual(hn, *_split_residual_refs(rest, n_gain))


def _ple_call(hb, ssq, gate_w, p, ple_w, layer, h, gains, name):
    m, d = h.shape
    pd = p.shape[-1]
    tm = _tile(m, 1024)
    tn = _tile(d, 512)
    n_gain = len(gains)
    gain_specs, out_specs, out_shape = _residual_specs(m, d, tm, tn, n_gain)
    blk = (tm * d * 2 + d * tn * 2 + tm * pd * 4 + pd * tn * 2
           + tm * tn * (8 + 2 * n_gain) + 2 * tm * LANES * 4)
    return pl.pallas_call(
        functools.partial(_ple_kernel, n_gain, d),
        grid=(m // tm, d // tn),
        in_specs=[pl.BlockSpec((tm, d), lambda i, j: (i, 0)),
                  pl.BlockSpec((tm, LANES), lambda i, j: (i, 0)),
                  pl.BlockSpec((None, d, tn), lambda i, j: (layer, 0, j)),
                  pl.BlockSpec((None, tm, pd), lambda i, j: (layer, i, 0)),
                  pl.BlockSpec((None, pd, tn), lambda i, j: (layer, 0, j)),
                  pl.BlockSpec((tm, tn), lambda i, j: (i, j))] + gain_specs,
        out_specs=out_specs,
        out_shape=out_shape,
        compiler_params=_params(blk, 4 * tm * tn * 4, ("parallel", "arbitrary")),
        name=name,
    )(hb, ssq, gate_w, p, ple_w, h, *gains)


def _attn_kernel(lambda_init, q_ref, k_ref, v_ref, gate_ref, lq1_ref, lk1_ref, lq2_ref,
                 lk2_ref, subg_ref, o_ref, m_sc, l_sc, acc_sc):
    tq = q_ref.shape[0]
    qi = pl.program_id(2)
    nt = (((1,), (1,)), ((), ()))

    m_sc[...] = jnp.full_like(m_sc, NEG_INF)
    l_sc[...] = jnp.zeros_like(l_sc)
    acc_sc[...] = jnp.zeros_like(acc_sc)

    def block(kv, masked):
        start = pl.multiple_of(kv * tq, tq)
        kb = k_ref[pl.ds(start, tq), :]
        vb = v_ref[pl.ds(start, tq), :]
        for c in range(2):
            qc = q_ref[:, c * HEAD_DIM:(c + 1) * HEAD_DIM]
            s = lax.dot_general(qc, kb[:, c * HEAD_DIM:(c + 1) * HEAD_DIM], nt,
                                preferred_element_type=F32)
            if masked:
                row = lax.broadcasted_iota(jnp.int32, s.shape, 0)
                col = lax.broadcasted_iota(jnp.int32, s.shape, 1)
                s = jnp.where(col <= row, s, NEG_INF)
            m_old = m_sc[c]
            m_new = jnp.maximum(m_old, jnp.max(s, axis=-1, keepdims=True))
            alpha = jnp.exp(m_old - m_new)
            p = jnp.exp(s - m_new)
            l_sc[c] = alpha * l_sc[c] + jnp.sum(p, axis=-1, keepdims=True)
            acc_sc[c] = alpha * acc_sc[c] + jnp.dot(p.astype(BF16), vb,
                                                    preferred_element_type=F32)
            m_sc[c] = m_new

    def body(kv, carry):
        block(kv, False)
        return carry

    lax.fori_loop(0, qi, body, 0)
    block(qi, True)

    lam = (jnp.exp(jnp.sum(lq1_ref[...] * lk1_ref[...], axis=-1, keepdims=True))
           - jnp.exp(jnp.sum(lq2_ref[...] * lk2_ref[...], axis=-1, keepdims=True))
           + lambda_init)
    o = acc_sc[0] / l_sc[0] - lam * (acc_sc[1] / l_sc[1])
    ms = jnp.mean(o * o, axis=-1, keepdims=True)
    on = o * lax.rsqrt(ms + RMS_EPS) * subg_ref[...] * (1.0 - lambda_init)
    o_ref[...] = (on * gate_ref[...].astype(F32)).astype(BF16)


def _attn_call(q, k, v, gate, lq1, lk1, lq2, lk2, sub_g, lambda_init, batch, name):
    m, width = q.shape
    seq = m // batch
    heads = width // V_DIM
    tq = _tile(seq, 512)
    nq = seq // tq
    qspec = pl.BlockSpec((tq, V_DIM), lambda b, h, i: (b * nq + i, h))
    kvspec = pl.BlockSpec((seq, V_DIM), lambda b, h, i: (b, h))
    lspec = pl.BlockSpec((1, HEAD_DIM), lambda b, h, i: (0, 0))
    blk = 3 * tq * V_DIM * 2 + 2 * seq * V_DIM * 2
    return pl.pallas_call(
        functools.partial(_attn_kernel, lambda_init),
        grid=(batch, heads, nq),
        in_specs=[qspec, kvspec, kvspec, qspec, lspec, lspec, lspec, lspec,
                  pl.BlockSpec((1, V_DIM), lambda b, h, i: (0, 0))],
        out_specs=qspec,
        out_shape=jax.ShapeDtypeStruct((m, width), BF16),
        scratch_shapes=[pltpu.VMEM((2, tq, 1), F32),
                        pltpu.VMEM((2, tq, 1), F32),
                        pltpu.VMEM((2, tq, V_DIM), F32)],
        compiler_params=_params(blk, 8 * tq * tq * 4 + 4 * tq * LANES * 4 * 2,
                                ("parallel", "parallel", "arbitrary")),
        name=name,
    )(q, k, v, gate, lq1, lk1, lq2, lk2, sub_g)


def kernel(x, p, a_norm_g, a_w_in, a_ln_g, a_ln_b, a_w_s, a_b_s, a_w_out, kv_norm_g, w_kv,
           k_norm_g, b_norm_g, b_w_in, b_q_norm_g, b_lam_q1, b_lam_k1, b_lam_q2, b_lam_k2,
           b_sub_norm_g, b_w_out, ple_w, ple_gate_norm_g, ple_gate_w):
    batch, seq, d = x.shape
    m = batch * seq
    depth = p.shape[0]
    n_a = a_w_in.shape[0]
    e = a_w_in.shape[2] // 3
    qk_width = b_w_in.shape[2] - b_w_out.shape[1]
    v_width = b_w_out.shape[1]
    assert seq % CHUNK == 0 and qk_width % V_DIM == 0 and v_width == qk_width

    row = lambda vec: vec.reshape(1, -1).astype(F32)
    a_w_in_b, a_w_out_b = a_w_in.astype(BF16), a_w_out.astype(BF16)
    w_kv_b, b_w_in_b, b_w_out_b = w_kv.astype(BF16), b_w_in.astype(BF16), b_w_out.astype(BF16)
    ple_w_b, ple_gate_w_b = ple_w.astype(BF16), ple_gate_w.astype(BF16)
    p2 = p.reshape(depth, m, p.shape[-1])

    h = x.reshape(m, d)
    hb, ssq = _prep_call(h, row(a_norm_g[0]))
    k_sh = v_sh = hb_kv = None

    for i in range(depth):
        tag = "L%d_" % i
        gate_gain = [row(ple_gate_norm_g[i])]
        if i < n_a:
            gu = _inproj_call(hb, ssq, a_w_in_b, i, 0, e, "gelu", name=tag + "in_u")
            gv, s1, s2 = _inproj_call(hb, ssq, a_w_in_b, i, e, e, "gelu_stats",
                                      name=tag + "in_v")
            sg = _inproj_call(hb, ssq, a_w_in_b, i, 2 * e, e, "silu", name=tag + "in_gate")
            y = _sgu_mix_call(gu, gv, sg, s1, s2, row(a_ln_g[i]), row(a_ln_b[i]), a_w_s[i],
                              a_b_s[i].reshape(a_b_s.shape[1], CHUNK, 1), tag + "sgu_mix")
            h, hb_g, ssq_g = _outproj_call(y, a_w_out_b, i, h, gate_gain, tag + "out")
        else:
            j = i - n_a
            if j == 0:
                k_sh = _inproj_call(hb_kv, ssq, w_kv_b, 0, 0, qk_width, "headnorm",
                                    head_gain=row(k_norm_g), name="kv_k")
                v_sh = _inproj_call(hb_kv, ssq, w_kv_b, 0, qk_width, v_width, "cast",
                                    name="kv_v")
            q = _inproj_call(hb, ssq, b_w_in_b, j, 0, qk_width, "headnorm",
                             head_gain=row(b_q_norm_g[j]), scale=HEAD_DIM ** -0.5,
                             name=tag + "in_q")
            sg = _inproj_call(hb, ssq, b_w_in_b, j, qk_width, v_width, "silu",
                              name=tag + "in_gate")
            ob = _attn_call(q, k_sh, v_sh, sg, row(b_lam_q1[j]), row(b_lam_k1[j]),
                            row(b_lam_q2[j]), row(b_lam_k2[j]), row(b_sub_norm_g[j]),
                            _lambda_init(i), batch, tag + "attn")
            h, hb_g, ssq_g = _outproj_call(ob, b_w_out_b, j, h, gate_gain, tag + "out")

        if i + 1 == depth:
            next_gains = []
        elif i + 1 < n_a:
            next_gains = [row(a_norm_g[i + 1])]
        elif i + 1 == n_a:
            next_gains = [row(b_norm_g[0]), row(kv_norm_g)]
        else:
            next_gains = [row(b_norm_g[i + 1 - n_a])]
        res = _ple_call(hb_g, ssq_g, ple_gate_w_b, p2, ple_w_b, i, h, next_gains, tag + "ple")
        h = res[0]
        if next_gains:
            hb, ssq = res[1], res[-1]
            if len(next_gains) == 2:
                hb_kv = res[2]

    return h.reshape(batch, seq, d)
```

```python
import functools
import math

import jax
import jax.numpy as jnp
from jax import lax
from jax.experimental import pallas as pl
from jax.experimental.pallas import tpu as pltpu

F32 = jnp.float32
BF16 = jnp.bfloat16

LANES = 128
V7X_VMEM_BYTES = 64 * 1024 * 1024
VMEM_LIMIT_CAP = 56 * 1024 * 1024

CHUNK = 128
HEAD_DIM = 128
V_DIM = 2 * HEAD_DIM
RMS_EPS = 1e-6
LN_EPS = 1e-5
NEG_INF = -1e30
LOG2_E = math.log2(math.e)


def _lambda_init(layer_idx):
    return 0.8 - 0.6 * math.exp(-0.3 * layer_idx)


def _tile(dim, pref):
    t = min(dim, pref)
    assert dim % t == 0, (dim, pref)
    return t


def _params(block_bytes, temp_bytes, semantics):
    need = 2 * block_bytes + temp_bytes + (4 << 20)
    return pltpu.CompilerParams(
        dimension_semantics=semantics,
        vmem_limit_bytes=int(min(max(need, 16 << 20), VMEM_LIMIT_CAP)))


def _row_rstd(ssq_ref, width):
    return lax.rsqrt(ssq_ref[:, :1] * (1.0 / width) + RMS_EPS)


def _prep_kernel(x_ref, g_ref, hb_ref, ssq_ref):
    x = x_ref[...]
    hb_ref[...] = (x * g_ref[...]).astype(BF16)
    ssq_ref[...] = jnp.broadcast_to(jnp.sum(x * x, axis=-1, keepdims=True), ssq_ref.shape)


def _prep_call(x, g):
    m, d = x.shape
    tm = _tile(m, 256)
    blk = tm * d * 6 + tm * LANES * 4
    return pl.pallas_call(
        _prep_kernel,
        grid=(m // tm,),
        in_specs=[pl.BlockSpec((tm, d), lambda i: (i, 0)),
                  pl.BlockSpec((1, d), lambda i: (0, 0))],
        out_specs=[pl.BlockSpec((tm, d), lambda i: (i, 0)),
                   pl.BlockSpec((tm, LANES), lambda i: (i, 0))],
        out_shape=[jax.ShapeDtypeStruct((m, d), BF16),
                   jax.ShapeDtypeStruct((m, LANES), F32)],
        compiler_params=_params(blk, tm * d * 8, ("arbitrary",)),
        name="prep_norm",
    )(x, g)


def _inproj_kernel(kind, d_in, scale, hb_ref, ssq_ref, w_ref, *rest):
    z = jnp.dot(hb_ref[...], w_ref[...], preferred_element_type=F32)
    z = z * _row_rstd(ssq_ref, d_in)
    if kind == "gelu":
        (o_ref,) = rest
        o_ref[...] = jax.nn.gelu(z).astype(BF16)
    elif kind == "gelu_stats":
        o_ref, s1_ref, s2_ref = rest
        g = jax.nn.gelu(z)
        o_ref[...] = g.astype(BF16)

        @pl.when(pl.program_id(1) == 0)
        def _():
            s1_ref[...] = jnp.zeros_like(s1_ref)
            s2_ref[...] = jnp.zeros_like(s2_ref)

        s1_ref[...] += jnp.sum(g, axis=-1, keepdims=True)
        s2_ref[...] += jnp.sum(g * g, axis=-1, keepdims=True)
    elif kind == "silu":
        (o_ref,) = rest
        o_ref[...] = (z * jax.nn.sigmoid(z)).astype(BF16)
    elif kind == "headnorm":
        g_ref, o_ref = rest
        gain = g_ref[...] * scale
        for c in range(z.shape[1] // HEAD_DIM):
            zc = z[:, c * HEAD_DIM:(c + 1) * HEAD_DIM]
            ms = jnp.mean(zc * zc, axis=-1, keepdims=True)
            o_ref[:, c * HEAD_DIM:(c + 1) * HEAD_DIM] = (
                zc * lax.rsqrt(ms + RMS_EPS) * gain).astype(BF16)
    else:
        assert kind == "cast"
        (o_ref,) = rest
        o_ref[...] = z.astype(BF16)


def _inproj_call(hb, ssq, w, layer, col0, ncols, kind, head_gain=None, scale=1.0, name=""):
    m, d = hb.shape
    tm = _tile(m, 1024)
    tn = _tile(ncols, 512)
    assert col0 % tn == 0
    off = col0 // tn
    if w.ndim == 3:
        w_spec = pl.BlockSpec((None, d, tn), lambda i, j: (layer, 0, j + off))
    else:
        w_spec = pl.BlockSpec((d, tn), lambda i, j: (0, j + off))
    in_specs = [pl.BlockSpec((tm, d), lambda i, j: (i, 0)),
                pl.BlockSpec((tm, LANES), lambda i, j: (i, 0)),
                w_spec]
    args = [hb, ssq, w]
    out_specs = [pl.BlockSpec((tm, tn), lambda i, j: (i, j))]
    out_shape = [jax.ShapeDtypeStruct((m, ncols), BF16)]
    if kind == "headnorm":
        in_specs.append(pl.BlockSpec((1, HEAD_DIM), lambda i, j: (0, 0)))
        args.append(head_gain)
    if kind == "gelu_stats":
        out_specs += [pl.BlockSpec((tm, LANES), lambda i, j: (i, 0))] * 2
        out_shape += [jax.ShapeDtypeStruct((m, LANES), F32)] * 2
    blk = tm * d * 2 + d * tn * 2 + tm * tn * 2 + 3 * tm * LANES * 4
    res = pl.pallas_call(
        functools.partial(_inproj_kernel, kind, d, scale),
        grid=(m // tm, ncols // tn),
        in_specs=in_specs,
        out_specs=out_specs,
        out_shape=out_shape,
        compiler_params=_params(blk, 4 * tm * tn * 4, ("parallel", "arbitrary")),
        name=name or ("inproj_" + kind),
    )(*args)
    return res if kind == "gelu_stats" else res[0]


def _sgu_mix_kernel(width, gu_ref, gv_ref, sg_ref, s1_ref, s2_ref, lng_ref, lnb_ref,
                    ws_ref, bs_ref, y_ref):
    tr, tc = gv_ref.shape
    gdim = tc // ws_ref.shape[0]
    mu = s1_ref[:, :1] * (1.0 / width)
    var = jnp.maximum(s2_ref[:, :1] * (1.0 / width) - mu * mu, 0.0)
    rstd = lax.rsqrt(var + LN_EPS)
    row = lax.broadcasted_iota(jnp.int32, (CHUNK, CHUNK), 0)
    col = lax.broadcasted_iota(jnp.int32, (CHUNK, CHUNK), 1)
    for g in range(ws_ref.shape[0]):
        cs = slice(g * gdim, (g + 1) * gdim)
        vn = ((gv_ref[:, cs].astype(F32) - mu) * rstd * lng_ref[:, cs]
              + lnb_ref[:, cs]).astype(BF16)
        wc = jnp.where(col <= row, ws_ref[g], 0.0).astype(BF16)
        for c in range(tr // CHUNK):
            rs = slice(c * CHUNK, (c + 1) * CHUNK)
            mixed = jnp.dot(wc, vn[rs], preferred_element_type=F32) + bs_ref[g]
            y_ref[rs, cs] = (gu_ref[rs, cs].astype(F32) * mixed
                             * sg_ref[rs, cs].astype(F32)).astype(BF16)


def _sgu_mix_call(gu, gv, sg, s1, s2, ln_g, ln_b, w_s, b_s, name):
    m, e = gv.shape
    groups = w_s.shape[0]
    gdim = e // groups
    gc = 2 if groups % 2 == 0 else 1
    tr = _tile(m, 512)
    tc = gc * gdim
    act = pl.BlockSpec((tr, tc), lambda i, j: (i, j))
    stat = pl.BlockSpec((tr, LANES), lambda i, j: (i, 0))
    vec = pl.BlockSpec((1, tc), lambda i, j: (0, j))
    blk = 4 * tr * tc * 2 + 2 * tr * LANES * 4 + 2 * tc * 4 + 2 * gc * CHUNK * LANES * 4
    return pl.pallas_call(
        functools.partial(_sgu_mix_kernel, e),
        grid=(m // tr, groups // gc),
        in_specs=[act, act, act, stat, stat, vec, vec,
                  pl.BlockSpec((gc, CHUNK, CHUNK), lambda i, j: (j, 0, 0)),
                  pl.BlockSpec((gc, CHUNK, 1), lambda i, j: (j, 0, 0))],
        out_specs=act,
        out_shape=jax.ShapeDtypeStruct((m, e), BF16),
        compiler_params=_params(blk, 6 * tr * tc * 4, ("parallel", "arbitrary")),
        name=name,
    )(gu, gv, sg, s1, s2, ln_g, ln_b, w_s, b_s)


def _emit_residual(hn, gain_refs, hnew_ref, hb_refs, ssq_ref):
    hnew_ref[...] = hn
    for g_ref, hb_ref in zip(gain_refs, hb_refs):
        hb_ref[...] = (hn * g_ref[...]).astype(BF16)
    if ssq_ref is not None:
        @pl.when(pl.program_id(1) == 0)
        def _():
            ssq_ref[...] = jnp.zeros_like(ssq_ref)

        ssq_ref[...] += jnp.sum(hn * hn, axis=-1, keepdims=True)


def _residual_specs(m, d, tm, tn, n_gain):
    gain_specs = [pl.BlockSpec((1, tn), lambda i, j: (0, j))] * n_gain
    out_specs = [pl.BlockSpec((tm, tn), lambda i, j: (i, j))] * (1 + n_gain)
    out_shape = [jax.ShapeDtypeStruct((m, d), F32)]
    out_shape += [jax.ShapeDtypeStruct((m, d), BF16)] * n_gain
    if n_gain:
        out_specs.append(pl.BlockSpec((tm, LANES), lambda i, j: (i, 0)))
        out_shape.append(jax.ShapeDtypeStruct((m, LANES), F32))
    return gain_specs, out_specs, out_shape


def _split_residual_refs(rest, n_gain):
    gain_refs = rest[:n_gain]
    hnew_ref = rest[n_gain]
    hb_refs = rest[n_gain + 1:2 * n_gain + 1]
    ssq_ref = rest[2 * n_gain + 1] if n_gain else None
    return gain_refs, hnew_ref, hb_refs, ssq_ref


def _outproj_kernel(n_gain, y_ref, w_ref, h_ref, *rest):
    hn = h_ref[...] + jnp.dot(y_ref[...], w_ref[...], preferred_element_type=F32)
    _emit_residual(hn, *_split_residual_refs(rest, n_gain))


def _outproj_call(y, w, layer, h, gains, name):
    m, k = y.shape
    d = h.shape[1]
    tm = _tile(m, 1024 if k <= 4096 else 512)
    tn = _tile(d, 512)
    n_gain = len(gains)
    gain_specs, out_specs, out_shape = _residual_specs(m, d, tm, tn, n_gain)
    blk = tm * k * 2 + k * tn * 2 + tm * tn * (8 + 2 * n_gain) + tm * LANES * 4
    return pl.pallas_call(
        functools.partial(_outproj_kernel, n_gain),
        grid=(m // tm, d // tn),
        in_specs=[pl.BlockSpec((tm, k), lambda i, j: (i, 0)),
                  pl.BlockSpec((None, k, tn), lambda i, j: (layer, 0, j)),
                  pl.BlockSpec((tm, tn), lambda i, j: (i, j))] + gain_specs,
        out_specs=out_specs,
        out_shape=out_shape,
        compiler_params=_params(blk, 3 * tm * tn * 4, ("parallel", "arbitrary")),
        name=name,
    )(y, w, h, *gains)


def _ple_kernel(n_gain, d_in, hb_ref, ssq_ref, gw_ref, p_ref, pw_ref, h_ref, *rest):
    logits = jnp.dot(hb_ref[...], gw_ref[...], preferred_element_type=F32)
    gate = jax.nn.sigmoid(logits * _row_rstd(ssq_ref, d_in))
    emb = jnp.dot(p_ref[...].astype(BF16), pw_ref[...], preferred_element_type=F32)
    hn = h_ref[...] + gate * emb
    _emit_residual(hn, *_split_residual_refs(rest, n_gain))


def _ple_call(hb, ssq, gate_w, p, ple_w, layer, h, gains, name):
    m, d = h.shape
    pd = p.shape[-1]
    tm = _tile(m, 1024)
    tn = _tile(d, 512)
    n_gain = len(gains)
    gain_specs, out_specs, out_shape = _residual_specs(m, d, tm, tn, n_gain)
    blk = (tm * d * 2 + d * tn * 2 + tm * pd * 4 + pd * tn * 2
           + tm * tn * (8 + 2 * n_gain) + 2 * tm * LANES * 4)
    return pl.pallas_call(
        functools.partial(_ple_kernel, n_gain, d),
        grid=(m // tm, d // tn),
        in_specs=[pl.BlockSpec((tm, d), lambda i, j: (i, 0)),
                  pl.BlockSpec((tm, LANES), lambda i, j: (i, 0)),
                  pl.BlockSpec((None, d, tn), lambda i, j: (layer, 0, j)),
                  pl.BlockSpec((None, tm, pd), lambda i, j: (layer, i, 0)),
                  pl.BlockSpec((None, pd, tn), lambda i, j: (layer, 0, j)),
                  pl.BlockSpec((tm, tn), lambda i, j: (i, j))] + gain_specs,
        out_specs=out_specs,
        out_shape=out_shape,
        compiler_params=_params(blk, 4 * tm * tn * 4, ("parallel", "arbitrary")),
        name=name,
    )(hb, ssq, gate_w, p, ple_w, h, *gains)


def _attn_kernel(lambda_init, q_ref, k_ref, v_ref, gate_ref, lq1_ref, lk1_ref, lq2_ref,
                 lk2_ref, subg_ref, o_ref, m_sc, l_sc, acc_sc, s_sc):
    tq = q_ref.shape[0]
    qi = pl.program_id(2)
    nt = (((1,), (1,)), ((), ()))

    m_sc[...] = jnp.full_like(m_sc, NEG_INF)
    l_sc[...] = jnp.zeros_like(l_sc)
    acc_sc[...] = jnp.zeros_like(acc_sc)

    def scores(kv, slot):
        start = pl.multiple_of(kv * tq, tq)
        kb = k_ref[pl.ds(start, tq), :]
        for c in range(2):
            qc = q_ref[:, c * HEAD_DIM:(c + 1) * HEAD_DIM]
            s_sc[slot, c] = lax.dot_general(qc, kb[:, c * HEAD_DIM:(c + 1) * HEAD_DIM], nt,
                                            preferred_element_type=F32)

    def consume(kv, slot, masked):
        start = pl.multiple_of(kv * tq, tq)
        vb = v_ref[pl.ds(start, tq), :]
        for c in range(2):
            s = s_sc[slot, c]
            if masked:
                row = lax.broadcasted_iota(jnp.int32, s.shape, 0)
                col = lax.broadcasted_iota(jnp.int32, s.shape, 1)
                s = jnp.where(col <= row, s, NEG_INF)
            m_prev = m_sc[c]
            m_next = jnp.maximum(m_prev, jnp.max(s, axis=1)[:, None])
            alpha = jnp.exp2(m_prev - m_next)
            p = jnp.exp2(s - pltpu.repeat(m_next, tq // LANES, 1))
            psum = p[:, :LANES]
            for t in range(1, tq // LANES):
                psum = psum + p[:, t * LANES:(t + 1) * LANES]
            l_sc[c] = alpha * l_sc[c] + psum
            acc_sc[c] = (pltpu.repeat(alpha, V_DIM // LANES, 1) * acc_sc[c]
                         + jnp.dot(p.astype(BF16), vb, preferred_element_type=F32))
            m_sc[c] = m_next

    def body(kv, carry):
        slot = lax.rem(kv, 2)
        consume(kv, slot, False)
        scores(kv + 1, 1 - slot)
        return carry

    scores(0, 0)
    lax.fori_loop(0, qi, body, 0)
    consume(qi, lax.rem(qi, 2), True)

    lam = (jnp.exp(jnp.sum(lq1_ref[...] * lk1_ref[...], axis=-1, keepdims=True))
           - jnp.exp(jnp.sum(lq2_ref[...] * lk2_ref[...], axis=-1, keepdims=True))
           + lambda_init)
    l1 = jnp.sum(l_sc[0], axis=1, keepdims=True)
    l2 = jnp.sum(l_sc[1], axis=1, keepdims=True)
    o = acc_sc[0] / l1 - lam * (acc_sc[1] / l2)
    ms = jnp.mean(o * o, axis=-1, keepdims=True)
    on = o * lax.rsqrt(ms + RMS_EPS) * subg_ref[...] * (1.0 - lambda_init)
    o_ref[...] = (on * gate_ref[...].astype(F32)).astype(BF16)


def _attn_call(q, k, v, gate, lq1, lk1, lq2, lk2, sub_g, lambda_init, batch, name):
    m, width = q.shape
    seq = m // batch
    heads = width // V_DIM
    tq = _tile(seq, 512)
    nq = seq // tq
    qspec = pl.BlockSpec((tq, V_DIM), lambda b, h, i: (b * nq + i, h))
    kvspec = pl.BlockSpec((seq, V_DIM), lambda b, h, i: (b, h))
    lspec = pl.BlockSpec((1, HEAD_DIM), lambda b, h, i: (0, 0))
    blk = 3 * tq * V_DIM * 2 + 2 * seq * V_DIM * 2
    return pl.pallas_call(
        functools.partial(_attn_kernel, lambda_init),
        grid=(batch, heads, nq),
        in_specs=[qspec, kvspec, kvspec, qspec, lspec, lspec, lspec, lspec,
                  pl.BlockSpec((1, V_DIM), lambda b, h, i: (0, 0))],
        out_specs=qspec,
        out_shape=jax.ShapeDtypeStruct((m, width), BF16),
        scratch_shapes=[pltpu.VMEM((2, tq, LANES), F32),
                        pltpu.VMEM((2, tq, LANES), F32),
                        pltpu.VMEM((2, tq, V_DIM), F32),
                        pltpu.VMEM((2, 2, tq, tq), F32)],
        compiler_params=_params(blk, 12 * tq * tq * 4 + 4 * tq * LANES * 4 * 2,
                                ("parallel", "parallel", "arbitrary")),
        name=name,
    )(q, k, v, gate, lq1, lk1, lq2, lk2, sub_g)


def kernel(x, p, a_norm_g, a_w_in, a_ln_g, a_ln_b, a_w_s, a_b_s, a_w_out, kv_norm_g, w_kv,
           k_norm_g, b_norm_g, b_w_in, b_q_norm_g, b_lam_q1, b_lam_k1, b_lam_q2, b_lam_k2,
           b_sub_norm_g, b_w_out, ple_w, ple_gate_norm_g, ple_gate_w):
    batch, seq, d = x.shape
    m = batch * seq
    depth = p.shape[0]
    n_a = a_w_in.shape[0]
    e = a_w_in.shape[2] // 3
    qk_width = b_w_in.shape[2] - b_w_out.shape[1]
    v_width = b_w_out.shape[1]
    assert seq % CHUNK == 0 and qk_width % V_DIM == 0 and v_width == qk_width

    row = lambda vec: vec.reshape(1, -1).astype(F32)
    a_w_in_b, a_w_out_b = a_w_in.astype(BF16), a_w_out.astype(BF16)
    w_kv_b, b_w_in_b, b_w_out_b = w_kv.astype(BF16), b_w_in.astype(BF16), b_w_out.astype(BF16)
    ple_w_b, ple_gate_w_b = ple_w.astype(BF16), ple_gate_w.astype(BF16)
    p2 = p.reshape(depth, m, p.shape[-1])

    h = x.reshape(m, d)
    hb, ssq = _prep_call(h, row(a_norm_g[0]))
    k_sh = v_sh = hb_kv = None

    for i in range(depth):
        tag = "L%d_" % i
        gate_gain = [row(ple_gate_norm_g[i])]
        if i < n_a:
            gu = _inproj_call(hb, ssq, a_w_in_b, i, 0, e, "gelu", name=tag + "in_u")
            gv, s1, s2 = _inproj_call(hb, ssq, a_w_in_b, i, e, e, "gelu_stats",
                                      name=tag + "in_v")
            sg = _inproj_call(hb, ssq, a_w_in_b, i, 2 * e, e, "silu", name=tag + "in_gate")
            y = _sgu_mix_call(gu, gv, sg, s1, s2, row(a_ln_g[i]), row(a_ln_b[i]), a_w_s[i],
                              a_b_s[i].reshape(a_b_s.shape[1], CHUNK, 1), tag + "sgu_mix")
            h, hb_g, ssq_g = _outproj_call(y, a_w_out_b, i, h, gate_gain, tag + "out")
        else:
            j = i - n_a
            if j == 0:
                k_sh = _inproj_call(hb_kv, ssq, w_kv_b, 0, 0, qk_width, "headnorm",
                                    head_gain=row(k_norm_g), name="kv_k")
                v_sh = _inproj_call(hb_kv, ssq, w_kv_b, 0, qk_width, v_width, "cast",
                                    name="kv_v")
            q = _inproj_call(hb, ssq, b_w_in_b, j, 0, qk_width, "headnorm",
                             head_gain=row(b_q_norm_g[j]), scale=HEAD_DIM ** -0.5 * LOG2_E,
                             name=tag + "in_q")
            sg = _inproj_call(hb, ssq, b_w_in_b, j, qk_width, v_width, "silu",
                              name=tag + "in_gate")
            ob = _attn_call(q, k_sh, v_sh, sg, row(b_lam_q1[j]), row(b_lam_k1[j]),
                            row(b_lam_q2[j]), row(b_lam_k2[j]), row(b_sub_norm_g[j]),
                            _lambda_init(i), batch, tag + "attn")
            h, hb_g, ssq_g = _outproj_call(ob, b_w_out_b, j, h, gate_gain, tag + "out")

        if i + 1 == depth:
            next_gains = []
        elif i + 1 < n_a:
            next_gains = [row(a_norm_g[i + 1])]
        elif i + 1 == n_a:
            next_gains = [row(b_norm_g[0]), row(kv_norm_g)]
        else:
            next_gains = [row(b_norm_g[i + 1 - n_a])]
        res = _ple_call(hb_g, ssq_g, ple_gate_w_b, p2, ple_w_b, i, h, next_gains, tag + "ple")
        h = res[0]
        if next_gains:
            hb, ssq = res[1], res[-1]
            if len(next_gains) == 2:
                hb_kv = res[2]

    return h.reshape(batch, seq, d)
```

```python
import functools
import math

import jax
import jax.numpy as jnp
from jax import lax
from jax.experimental import pallas as pl
from jax.experimental.pallas import tpu as pltpu

F32 = jnp.float32
BF16 = jnp.bfloat16

LANES = 128
ROW_CHUNK = 512
V7X_VMEM_BYTES = 64 * 1024 * 1024
VMEM_LIMIT_CAP = 56 * 1024 * 1024

CHUNK = 128
HEAD_DIM = 128
V_DIM = 2 * HEAD_DIM
RMS_EPS = 1e-6
LN_EPS = 1e-5
NEG_INF = -1e30
LOG2_E = math.log2(math.e)


def _lambda_init(layer_idx):
    return 0.8 - 0.6 * math.exp(-0.3 * layer_idx)


def _tile(dim, pref):
    t = min(dim, pref)
    assert dim % t == 0, (dim, pref)
    return t


def _params(block_bytes, temp_bytes, semantics):
    need = 2 * block_bytes + temp_bytes + (4 << 20)
    return pltpu.CompilerParams(
        dimension_semantics=semantics,
        vmem_limit_bytes=int(min(max(need, 16 << 20), VMEM_LIMIT_CAP)))


def _row_chunks(n_rows):
    rows = min(ROW_CHUNK, n_rows)
    return [slice(r0, r0 + rows) for r0 in range(0, n_rows, rows)]


def _prep_kernel(x_ref, g_ref, hb_ref, ssq_ref):
    x = x_ref[...]
    hb_ref[...] = (x * g_ref[...]).astype(BF16)
    ssq_ref[...] = jnp.broadcast_to(jnp.sum(x * x, axis=-1, keepdims=True), ssq_ref.shape)


def _prep_call(x, g):
    m, d = x.shape
    tm = _tile(m, 256)
    blk = tm * d * 6 + tm * LANES * 4
    return pl.pallas_call(
        _prep_kernel,
        grid=(m // tm,),
        in_specs=[pl.BlockSpec((tm, d), lambda i: (i, 0)),
                  pl.BlockSpec((1, d), lambda i: (0, 0))],
        out_specs=[pl.BlockSpec((tm, d), lambda i: (i, 0)),
                   pl.BlockSpec((tm, LANES), lambda i: (i, 0))],
        out_shape=[jax.ShapeDtypeStruct((m, d), BF16),
                   jax.ShapeDtypeStruct((m, LANES), F32)],
        compiler_params=_params(blk, tm * d * 8, ("arbitrary",)),
        name="prep_norm",
    )(x, g)


def _inproj_kernel(kind, d_in, scale, hb_ref, ssq_ref, w_ref, *rest):
    o_ref = rest[1] if kind == "headnorm" else rest[0]
    if kind == "gelu_stats":
        s1_ref, s2_ref = rest[1:]

        @pl.when(pl.program_id(1) == 0)
        def _():
            s1_ref[...] = jnp.zeros_like(s1_ref)
            s2_ref[...] = jnp.zeros_like(s2_ref)

    for rs in _row_chunks(o_ref.shape[0]):
        rstd = lax.rsqrt(ssq_ref[rs, :1] * (1.0 / d_in) + RMS_EPS)
        z = jnp.dot(hb_ref[rs, :], w_ref[...], preferred_element_type=F32) * rstd
        if kind == "gelu":
            o_ref[rs, :] = jax.nn.gelu(z).astype(BF16)
        elif kind == "gelu_stats":
            g = jax.nn.gelu(z)
            o_ref[rs, :] = g.astype(BF16)
            s1_ref[rs, :] += jnp.sum(g, axis=-1, keepdims=True)
            s2_ref[rs, :] += jnp.sum(g * g, axis=-1, keepdims=True)
        elif kind == "silu":
            o_ref[rs, :] = (z * jax.nn.sigmoid(z)).astype(BF16)
        elif kind == "headnorm":
            gain = rest[0][...] * scale
            for h0 in range(0, z.shape[1], HEAD_DIM):
                zc = z[:, h0:h0 + HEAD_DIM]
                ms = jnp.mean(zc * zc, axis=-1, keepdims=True)
                o_ref[rs, h0:h0 + HEAD_DIM] = (
                    zc * lax.rsqrt(ms + RMS_EPS) * gain).astype(BF16)
        else:
            assert kind == "cast"
            o_ref[rs, :] = z.astype(BF16)


def _inproj_call(hb, ssq, w, layer, col0, ncols, kind, head_gain=None, scale=1.0, name=""):
    m, d = hb.shape
    tm = _tile(m, 2048)
    tn = _tile(ncols, 512)
    assert col0 % tn == 0
    off = col0 // tn
    if w.ndim == 3:
        w_spec = pl.BlockSpec((None, d, tn), lambda i, j: (layer, 0, j + off))
    else:
        w_spec = pl.BlockSpec((d, tn), lambda i, j: (0, j + off))
    in_specs = [pl.BlockSpec((tm, d), lambda i, j: (i, 0)),
                pl.BlockSpec((tm, LANES), lambda i, j: (i, 0)),
                w_spec]
    args = [hb, ssq, w]
    out_specs = [pl.BlockSpec((tm, tn), lambda i, j: (i, j))]
    out_shape = [jax.ShapeDtypeStruct((m, ncols), BF16)]
    if kind == "headnorm":
        in_specs.append(pl.BlockSpec((1, HEAD_DIM), lambda i, j: (0, 0)))
        args.append(head_gain)
    if kind == "gelu_stats":
        out_specs += [pl.BlockSpec((tm, LANES), lambda i, j: (i, 0))] * 2
        out_shape += [jax.ShapeDtypeStruct((m, LANES), F32)] * 2
    blk = tm * d * 2 + d * tn * 2 + tm * tn * 2 + 3 * tm * LANES * 4
    res = pl.pallas_call(
        functools.partial(_inproj_kernel, kind, d, scale),
        grid=(m // tm, ncols // tn),
        in_specs=in_specs,
        out_specs=out_specs,
        out_shape=out_shape,
        compiler_params=_params(blk, 4 * tm * tn * 4, ("parallel", "arbitrary")),
        name=name or ("inproj_" + kind),
    )(*args)
    return res if kind == "gelu_stats" else res[0]


def _sgu_mix_kernel(width, gu_ref, gv_ref, sg_ref, s1_ref, s2_ref, lng_ref, lnb_ref,
                    ws_ref, bs_ref, y_ref):
    tr, tc = gv_ref.shape
    gdim = tc // ws_ref.shape[0]
    mu = s1_ref[:, :1] * (1.0 / width)
    var = jnp.maximum(s2_ref[:, :1] * (1.0 / width) - mu * mu, 0.0)
    rstd = lax.rsqrt(var + LN_EPS)
    row = lax.broadcasted_iota(jnp.int32, (CHUNK, CHUNK), 0)
    col = lax.broadcasted_iota(jnp.int32, (CHUNK, CHUNK), 1)
    for g in range(ws_ref.shape[0]):
        cs = slice(g * gdim, (g + 1) * gdim)
        vn = ((gv_ref[:, cs].astype(F32) - mu) * rstd * lng_ref[:, cs]
              + lnb_ref[:, cs]).astype(BF16)
        wc = jnp.where(col <= row, ws_ref[g], 0.0).astype(BF16)
        for c in range(tr // CHUNK):
            rs = slice(c * CHUNK, (c + 1) * CHUNK)
            mixed = jnp.dot(wc, vn[rs], preferred_element_type=F32) + bs_ref[g]
            y_ref[rs, cs] = (gu_ref[rs, cs].astype(F32) * mixed
                             * sg_ref[rs, cs].astype(F32)).astype(BF16)


def _sgu_mix_call(gu, gv, sg, s1, s2, ln_g, ln_b, w_s, b_s, name):
    m, e = gv.shape
    groups = w_s.shape[0]
    gdim = e // groups
    gc = 2 if groups % 2 == 0 else 1
    tr = _tile(m, 512)
    tc = gc * gdim
    act = pl.BlockSpec((tr, tc), lambda i, j: (i, j))
    stat = pl.BlockSpec((tr, LANES), lambda i, j: (i, 0))
    vec = pl.BlockSpec((1, tc), lambda i, j: (0, j))
    blk = 4 * tr * tc * 2 + 2 * tr * LANES * 4 + 2 * tc * 4 + 2 * gc * CHUNK * LANES * 4
    return pl.pallas_call(
        functools.partial(_sgu_mix_kernel, e),
        grid=(m // tr, groups // gc),
        in_specs=[act, act, act, stat, stat, vec, vec,
                  pl.BlockSpec((gc, CHUNK, CHUNK), lambda i, j: (j, 0, 0)),
                  pl.BlockSpec((gc, CHUNK, 1), lambda i, j: (j, 0, 0))],
        out_specs=act,
        out_shape=jax.ShapeDtypeStruct((m, e), BF16),
        compiler_params=_params(blk, 6 * tr * tc * 4, ("parallel", "arbitrary")),
        name=name,
    )(gu, gv, sg, s1, s2, ln_g, ln_b, w_s, b_s)


def _init_residual(ssq_ref):
    if ssq_ref is not None:
        @pl.when(pl.program_id(1) == 0)
        def _():
            ssq_ref[...] = jnp.zeros_like(ssq_ref)


def _emit_residual(rs, hn, gain_refs, hnew_ref, hb_refs, ssq_ref):
    hnew_ref[rs, :] = hn
    for g_ref, hb_ref in zip(gain_refs, hb_refs):
        hb_ref[rs, :] = (hn * g_ref[...]).astype(BF16)
    if ssq_ref is not None:
        ssq_ref[rs, :] += jnp.sum(hn * hn, axis=-1, keepdims=True)


def _residual_specs(m, d, tm, tn, n_gain):
    gain_specs = [pl.BlockSpec((1, tn), lambda i, j: (0, j))] * n_gain
    out_specs = [pl.BlockSpec((tm, tn), lambda i, j: (i, j))] * (1 + n_gain)
    out_shape = [jax.ShapeDtypeStruct((m, d), F32)]
    out_shape += [jax.ShapeDtypeStruct((m, d), BF16)] * n_gain
    if n_gain:
        out_specs.append(pl.BlockSpec((tm, LANES), lambda i, j: (i, 0)))
        out_shape.append(jax.ShapeDtypeStruct((m, LANES), F32))
    return gain_specs, out_specs, out_shape


def _split_residual_refs(rest, n_gain):
    gain_refs = rest[:n_gain]
    hnew_ref = rest[n_gain]
    hb_refs = rest[n_gain + 1:2 * n_gain + 1]
    ssq_ref = rest[2 * n_gain + 1] if n_gain else None
    return gain_refs, hnew_ref, hb_refs, ssq_ref


def _outproj_kernel(n_gain, y_ref, w_ref, h_ref, *rest):
    refs = _split_residual_refs(rest, n_gain)
    _init_residual(refs[-1])
    for rs in _row_chunks(y_ref.shape[0]):
        hn = h_ref[rs, :] + jnp.dot(y_ref[rs, :], w_ref[...], preferred_element_type=F32)
        _emit_residual(rs, hn, *refs)


def _outproj_call(y, w, layer, h, gains, name):
    m, k = y.shape
    d = h.shape[1]
    tm = _tile(m, 1024)
    tn = _tile(d, 512 if k <= 4096 else 256)
    n_gain = len(gains)
    gain_specs, out_specs, out_shape = _residual_specs(m, d, tm, tn, n_gain)
    blk = tm * k * 2 + k * tn * 2 + tm * tn * (8 + 2 * n_gain) + tm * LANES * 4
    return pl.pallas_call(
        functools.partial(_outproj_kernel, n_gain),
        grid=(m // tm, d // tn),
        in_specs=[pl.BlockSpec((tm, k), lambda i, j: (i, 0)),
                  pl.BlockSpec((None, k, tn), lambda i, j: (layer, 0, j)),
                  pl.BlockSpec((tm, tn), lambda i, j: (i, j))] + gain_specs,
        out_specs=out_specs,
        out_shape=out_shape,
        compiler_params=_params(blk, 3 * tm * tn * 4, ("parallel", "arbitrary")),
        name=name,
    )(y, w, h, *gains)


def _ple_kernel(n_gain, d_in, hb_ref, ssq_ref, gw_ref, p_ref, pw_ref, h_ref, *rest):
    refs = _split_residual_refs(rest, n_gain)
    _init_residual(refs[-1])
    for rs in _row_chunks(hb_ref.shape[0]):
        rstd = lax.rsqrt(ssq_ref[rs, :1] * (1.0 / d_in) + RMS_EPS)
        logits = jnp.dot(hb_ref[rs, :], gw_ref[...], preferred_element_type=F32)
        gate = jax.nn.sigmoid(logits * rstd)
        emb = jnp.dot(p_ref[rs, :].astype(BF16), pw_ref[...], preferred_element_type=F32)
        hn = h_ref[rs, :] + gate * emb
        _emit_residual(rs, hn, *refs)


def _ple_call(hb, ssq, gate_w, p, ple_w, layer, h, gains, name):
    m, d = h.shape
    pd = p.shape[-1]
    tm = _tile(m, 1024)
    tn = _tile(d, 512)
    n_gain = len(gains)
    gain_specs, out_specs, out_shape = _residual_specs(m, d, tm, tn, n_gain)
    blk = (tm * d * 2 + d * tn * 2 + tm * pd * 4 + pd * tn * 2
           + tm * tn * (8 + 2 * n_gain) + 2 * tm * LANES * 4)
    return pl.pallas_call(
        functools.partial(_ple_kernel, n_gain, d),
        grid=(m // tm, d // tn),
        in_specs=[pl.BlockSpec((tm, d), lambda i, j: (i, 0)),
                  pl.BlockSpec((tm, LANES), lambda i, j: (i, 0)),
                  pl.BlockSpec((None, d, tn), lambda i, j: (layer, 0, j)),
                  pl.BlockSpec((None, tm, pd), lambda i, j: (layer, i, 0)),
                  pl.BlockSpec((None, pd, tn), lambda i, j: (layer, 0, j)),
                  pl.BlockSpec((tm, tn), lambda i, j: (i, j))] + gain_specs,
        out_specs=out_specs,
        out_shape=out_shape,
        compiler_params=_params(blk, 4 * tm * tn * 4, ("parallel", "arbitrary")),
        name=name,
    )(hb, ssq, gate_w, p, ple_w, h, *gains)


def _attn_kernel(lambda_init, q_ref, k_ref, v_ref, gate_ref, lq1_ref, lk1_ref, lq2_ref,
                 lk2_ref, subg_ref, o_ref, m_sc, l_sc, acc_sc, s_sc):
    tq = q_ref.shape[0]
    qi = pl.program_id(2)
    nt = (((1,), (1,)), ((), ()))

    m_sc[...] = jnp.full_like(m_sc, NEG_INF)
    l_sc[...] = jnp.zeros_like(l_sc)
    acc_sc[...] = jnp.zeros_like(acc_sc)

    def scores(kv, slot):
        start = pl.multiple_of(kv * tq, tq)
        kb = k_ref[pl.ds(start, tq), :]
        for c in range(2):
            qc = q_ref[:, c * HEAD_DIM:(c + 1) * HEAD_DIM]
            s_sc[slot, c] = lax.dot_general(qc, kb[:, c * HEAD_DIM:(c + 1) * HEAD_DIM], nt,
                                            preferred_element_type=F32)

    def consume(kv, slot, masked):
        start = pl.multiple_of(kv * tq, tq)
        vb = v_ref[pl.ds(start, tq), :]
        for c in range(2):
            s = s_sc[slot, c]
            if masked:
                row = lax.broadcasted_iota(jnp.int32, s.shape, 0)
                col = lax.broadcasted_iota(jnp.int32, s.shape, 1)
                s = jnp.where(col <= row, s, NEG_INF)
            m_prev = m_sc[c]
            m_next = jnp.maximum(m_prev, jnp.max(s, axis=1)[:, None])
            alpha = jnp.exp2(m_prev - m_next)
            p = jnp.exp2(s - jnp.tile(m_next, (1, tq // LANES)))
            psum = p[:, :LANES]
            for t in range(1, tq // LANES):
                psum = psum + p[:, t * LANES:(t + 1) * LANES]
            l_sc[c] = alpha * l_sc[c] + psum
            acc_sc[c] = (jnp.tile(alpha, (1, V_DIM // LANES)) * acc_sc[c]
                         + jnp.dot(p.astype(BF16), vb, preferred_element_type=F32))
            m_sc[c] = m_next

    def body(kv, carry):
        slot = lax.rem(kv, 2)
        consume(kv, slot, False)
        scores(kv + 1, 1 - slot)
        return carry

    scores(0, 0)
    lax.fori_loop(0, qi, body, 0)
    consume(qi, lax.rem(qi, 2), True)

    lam = (jnp.exp(jnp.sum(lq1_ref[...] * lk1_ref[...], axis=-1, keepdims=True))
           - jnp.exp(jnp.sum(lq2_ref[...] * lk2_ref[...], axis=-1, keepdims=True))
           + lambda_init)
    l1 = jnp.sum(l_sc[0], axis=1, keepdims=True)
    l2 = jnp.sum(l_sc[1], axis=1, keepdims=True)
    o = acc_sc[0] / l1 - lam * (acc_sc[1] / l2)
    ms = jnp.mean(o * o, axis=-1, keepdims=True)
    on = o * lax.rsqrt(ms + RMS_EPS) * subg_ref[...] * (1.0 - lambda_init)
    o_ref[...] = (on * gate_ref[...].astype(F32)).astype(BF16)


def _attn_call(q, k, v, gate, lq1, lk1, lq2, lk2, sub_g, lambda_init, batch, name):
    m, width = q.shape
    seq = m // batch
    heads = width // V_DIM
    tq = _tile(seq, 512)
    nq = seq // tq
    qspec = pl.BlockSpec((tq, V_DIM), lambda b, h, i: (b * nq + i, h))
    kvspec = pl.BlockSpec((seq, V_DIM), lambda b, h, i: (b, h))
    lspec = pl.BlockSpec((1, HEAD_DIM), lambda b, h, i: (0, 0))
    blk = 3 * tq * V_DIM * 2 + 2 * seq * V_DIM * 2
    return pl.pallas_call(
        functools.partial(_attn_kernel, lambda_init),
        grid=(batch, heads, nq),
        in_specs=[qspec, kvspec, kvspec, qspec, lspec, lspec, lspec, lspec,
                  pl.BlockSpec((1, V_DIM), lambda b, h, i: (0, 0))],
        out_specs=qspec,
        out_shape=jax.ShapeDtypeStruct((m, width), BF16),
        scratch_shapes=[pltpu.VMEM((2, tq, LANES), F32),
                        pltpu.VMEM((2, tq, LANES), F32),
                        pltpu.VMEM((2, tq, V_DIM), F32),
                        pltpu.VMEM((2, 2, tq, tq), F32)],
        compiler_params=_params(blk, 12 * tq * tq * 4 + 4 * tq * LANES * 4 * 2,
                                ("parallel", "parallel", "arbitrary")),
        name=name,
    )(q, k, v, gate, lq1, lk1, lq2, lk2, sub_g)


def kernel(x, p, a_norm_g, a_w_in, a_ln_g, a_ln_b, a_w_s, a_b_s, a_w_out, kv_norm_g, w_kv,
           k_norm_g, b_norm_g, b_w_in, b_q_norm_g, b_lam_q1, b_lam_k1, b_lam_q2, b_lam_k2,
           b_sub_norm_g, b_w_out, ple_w, ple_gate_norm_g, ple_gate_w):
    batch, seq, d = x.shape
    m = batch * seq
    depth = p.shape[0]
    n_a = a_w_in.shape[0]
    e = a_w_in.shape[2] // 3
    qk_width = b_w_in.shape[2] - b_w_out.shape[1]
    v_width = b_w_out.shape[1]
    assert seq % CHUNK == 0 and qk_width % V_DIM == 0 and v_width == qk_width

    row = lambda vec: vec.reshape(1, -1).astype(F32)
    a_w_in_b, a_w_out_b = a_w_in.astype(BF16), a_w_out.astype(BF16)
    w_kv_b, b_w_in_b, b_w_out_b = w_kv.astype(BF16), b_w_in.astype(BF16), b_w_out.astype(BF16)
    ple_w_b, ple_gate_w_b = ple_w.astype(BF16), ple_gate_w.astype(BF16)
    p2 = p.reshape(depth, m, p.shape[-1])

    h = x.reshape(m, d)
    hb, ssq = _prep_call(h, row(a_norm_g[0]))
    k_sh = v_sh = hb_kv = None

    for i in range(depth):
        tag = "L%d_" % i
        gate_gain = [row(ple_gate_norm_g[i])]
        if i < n_a:
            gu = _inproj_call(hb, ssq, a_w_in_b, i, 0, e, "gelu", name=tag + "in_u")
            gv, s1, s2 = _inproj_call(hb, ssq, a_w_in_b, i, e, e, "gelu_stats",
                                      name=tag + "in_v")
            sg = _inproj_call(hb, ssq, a_w_in_b, i, 2 * e, e, "silu", name=tag + "in_gate")
            y = _sgu_mix_call(gu, gv, sg, s1, s2, row(a_ln_g[i]), row(a_ln_b[i]), a_w_s[i],
                              a_b_s[i].reshape(a_b_s.shape[1], CHUNK, 1), tag + "sgu_mix")
            h, hb_g, ssq_g = _outproj_call(y, a_w_out_b, i, h, gate_gain, tag + "out")
        else:
            j = i - n_a
            if j == 0:
                k_sh = _inproj_call(hb_kv, ssq, w_kv_b, 0, 0, qk_width, "headnorm",
                                    head_gain=row(k_norm_g), name="kv_k")
                v_sh = _inproj_call(hb_kv, ssq, w_kv_b, 0, qk_width, v_width, "cast",
                                    name="kv_v")
            q = _inproj_call(hb, ssq, b_w_in_b, j, 0, qk_width, "headnorm",
                             head_gain=row(b_q_norm_g[j]), scale=HEAD_DIM ** -0.5 * LOG2_E,
                             name=tag + "in_q")
            sg = _inproj_call(hb, ssq, b_w_in_b, j, qk_width, v_width, "silu",
                              name=tag + "in_gate")
            ob = _attn_call(q, k_sh, v_sh, sg, row(b_lam_q1[j]), row(b_lam_k1[j]),
                            row(b_lam_q2[j]), row(b_lam_k2[j]), row(b_sub_norm_g[j]),
                            _lambda_init(i), batch, tag + "attn")
            h, hb_g, ssq_g = _outproj_call(ob, b_w_out_b, j, h, gate_gain, tag + "out")

        if i + 1 == depth:
            next_gains = []
        elif i + 1 < n_a:
            next_gains = [row(a_norm_g[i + 1])]
        elif i + 1 == n_a:
            next_gains = [row(b_norm_g[0]), row(kv_norm_g)]
        else:
            next_gains = [row(b_norm_g[i + 1 - n_a])]
        res = _ple_call(hb_g, ssq_g, ple_gate_w_b, p2, ple_w_b, i, h, next_gains, tag + "ple")
        h = res[0]
        if next_gains:
            hb, ssq = res[1], res[-1]
            if len(next_gains) == 2:
                hb_kv = res[2]

    return h.reshape(batch, seq, d)
```
